```python
import math
import jax, jax.numpy as jnp
from jax import lax
import numpy as np

D_MODEL = 1024
BATCH = 4
SEQ = 4096
DEPTH = 4
DEC_BATCH = 128
DEC_SEQ = 1
PAST_LEN = 2048
PAGE_SIZE = 128

N_MIXERS = 2
N_A_LAYERS = (DEPTH + 1) // 2
N_B_LAYERS = DEPTH // 2

RWKV_HEAD = 64
RWKV_HEADS = D_MODEL // RWKV_HEAD
DECAY_LORA = 64
AAA_LORA = 64
MV_LORA = 32
GATE_LORA = 128
GN_EPS = 64e-5

MOBA_HEAD = 64
MOBA_HEADS = D_MODEL // MOBA_HEAD
MOBA_BLOCK = 256
MOBA_TOPK = 3
MOBA_Q_CHUNK = 16

REL_BUCKETS = 32
REL_MAX_DIST = 128

N_GROUPS = 4
EXPERTS_PER_GROUP = 8
N_EXPERTS = N_GROUPS * EXPERTS_PER_GROUP
D_EXPERT = 256
TOPK_IN_GROUP = 2

LN_EPS = 1e-5
NEG_INF = -1e30
DEEPNORM_ALPHA = (2 * DEPTH) ** 0.25
DEEPNORM_BETA = (8 * DEPTH) ** -0.25

kernel_name = 'rwkv7_moba_hmoe_deepnorm_step'


def layer_norm(x, g, b):
    xf = x.astype(jnp.float32)
    mu = jnp.mean(xf, -1, keepdims=True)
    var = jnp.mean(jnp.square(xf - mu), -1, keepdims=True)
    return ((xf - mu) * lax.rsqrt(var + LN_EPS)).astype(x.dtype) * g + b


def rwkv7_mixer(x, shift_prev, s0, v_first, vres, mu, w_rkv, w0, w1, w2, a0, a1, a2,
                g1, g2, k_k, k_a, r_k, gn_g, gn_b, w_o):
    B, T, D = x.shape
    H, N = RWKV_HEADS, RWKV_HEAD
    f32 = jnp.float32
    x_prev = jnp.concatenate([shift_prev[:, None, :].astype(x.dtype), x[:, :-1]], axis=1)
    xs = x[None] + (x_prev - x)[None] * mu[:, None, None, :]
    rkv = jnp.einsum('pbtd,pde->pbte', xs[:3], w_rkv)
    r, k, v = rkv[0].astype(f32), rkv[1].astype(f32), rkv[2].astype(f32)
    w = -jax.nn.softplus(-(w0 + jnp.tanh(xs[3] @ w1) @ w2).astype(f32)) - 0.5
    decay = jnp.exp(-jnp.exp(w))
    a = jax.nn.sigmoid((a0 + (xs[4] @ a1) @ a2).astype(f32))
    g = jax.nn.sigmoid(xs[5] @ g1) @ g2
    if vres is None:
        v_first = v
    else:
        v0, v1, v2 = vres
        v = v + (v_first - v) * jax.nn.sigmoid((v0 + (xs[2] @ v1) @ v2).astype(f32))
    heads = lambda t: t.reshape(B, T, H, N)
    kk = heads(k * k_k.astype(f32))
    kk = kk * lax.rsqrt(jnp.maximum(jnp.sum(kk * kk, -1, keepdims=True), 1e-24))
    k = k * (1.0 + (a - 1.0) * k_a.astype(f32))
    r_h, w_h, k_h, v_h, a_h = heads(r), heads(decay), heads(k), heads(v), heads(a)

    def step(S, inp):
        r_t, w_t, k_t, v_t, kk_t, a_t = inp
        sk = jnp.einsum('bhvk,bhk->bhv', S, kk_t)
        S = (S * w_t[:, :, None, :] - sk[..., None] * (kk_t * a_t)[:, :, None, :]
             + v_t[..., None] * k_t[:, :, None, :])
        return S, jnp.einsum('bhvk,bhk->bhv', S, r_t)

    seq = tuple(jnp.moveaxis(t, 1, 0) for t in (r_h, w_h, k_h, v_h, kk, a_h))
    S_T, o = lax.scan(step, s0.astype(f32), seq)
    o = jnp.moveaxis(o, 0, 1)
    m = jnp.mean(o, -1, keepdims=True)
    var = jnp.mean(jnp.square(o - m), -1, keepdims=True)
    o = ((o - m) * lax.rsqrt(var + GN_EPS)).reshape(B, T, D) * gn_g + gn_b
    bonus = (jnp.sum(r_h * k_h * r_k.astype(f32), -1, keepdims=True) * v_h).reshape(B, T, D)
    out = ((o + bonus) * g) @ w_o
    return out.astype(x.dtype), v_first, S_T.astype(s0.dtype), x[:, -1]


def t5_bucket(rel):
    n = jnp.maximum(rel, 0)
    max_exact = REL_BUCKETS // 2
    large = max_exact + (jnp.log(jnp.maximum(n, max_exact).astype(jnp.float32) / max_exact)
                         / math.log(REL_MAX_DIST / max_exact) * (REL_BUCKETS - max_exact)).astype(jnp.int32)
    large = jnp.minimum(large, REL_BUCKETS - 1)
    return jnp.where(n < max_exact, n, large)


def moba_mixer(x, k_past, v_past, w_qkv, w_o, rel_bias):
    B, T, _ = x.shape
    H, Dh = MOBA_HEADS, MOBA_HEAD
    f32 = jnp.float32
    P = k_past.shape[1]
    qkv = (x @ w_qkv).reshape(B, T, 3, H, Dh)
    q, k_new, v_new = qkv[:, :, 0], qkv[:, :, 1], qkv[:, :, 2]
    L = P + T
    NB = -(-L // MOBA_BLOCK)
    pad = jnp.zeros((B, NB * MOBA_BLOCK - L, H, Dh), x.dtype)
    k_all = jnp.concatenate([k_past.astype(x.dtype), k_new, pad], axis=1).reshape(B, NB, MOBA_BLOCK, H, Dh)
    v_all = jnp.concatenate([v_past.astype(x.dtype), v_new, pad], axis=1).reshape(B, NB, MOBA_BLOCK, H, Dh)
    k_mean = jnp.mean(k_all.astype(f32), axis=2)
    qc = min(MOBA_Q_CHUNK, T)
    n_chunks = -(-T // qc)
    Tp = n_chunks * qc
    q = jnp.pad(q, ((0, 0), (0, Tp - T), (0, 0), (0, 0)))
    q_chunks = q.reshape(B, n_chunks, qc, H, Dh).transpose(1, 0, 3, 2, 4)
    pos_chunks = (P + jnp.arange(Tp, dtype=jnp.int32)).reshape(n_chunks, qc)
    kt = min(MOBA_TOPK, NB)
    b_idx = jnp.arange(B)[:, None, None, None]
    h_idx = jnp.arange(H)[None, :, None, None]
    blk_ids = jnp.arange(NB, dtype=jnp.int32)
    key_off = jnp.arange(MOBA_BLOCK, dtype=jnp.int32)
    bias_ht = rel_bias.T.astype(f32)
    scale = MOBA_HEAD ** -0.5

    def attend(args):
        qb, pos = args
        own = pos // MOBA_BLOCK
        gate = jnp.einsum('bhqd,bnhd->bhqn', qb.astype(f32), k_mean)
        gate = jnp.where(blk_ids[None, :] < own[:, None], gate, NEG_INF)
        _, top = lax.top_k(gate, kt)
        own_b = jnp.broadcast_to(own[None, None, :, None], (B, H, qc, 1))
        sel = jnp.concatenate([top, own_b], -1)
        valid = jnp.concatenate([top < own[None, None, :, None], jnp.ones(own_b.shape, bool)], -1)
        k_sel = k_all[b_idx, sel, :, h_idx, :]
        v_sel = v_all[b_idx, sel, :, h_idx, :]
        kpos = sel[..., None] * MOBA_BLOCK + key_off
        rel = pos[None, None, :, None, None] - kpos
        bias = bias_ht[h_idx[..., None], t5_bucket(rel)]
        logits = jnp.einsum('bhqd,bhqskd->bhqsk', qb, k_sel).astype(f32) * scale + bias
        logits = jnp.where(valid[..., None] & (rel >= 0), logits, NEG_INF)
        p = jax.nn.softmax(logits.reshape(B, H, qc, -1), axis=-1).reshape(logits.shape)
        return jnp.einsum('bhqsk,bhqskd->bhqd', p.astype(v_sel.dtype), v_sel)

    o = lax.map(attend, (q_chunks, pos_chunks))
    o = o.transpose(1, 0, 3, 2, 4).reshape(B, Tp, H * Dh)[:, :T]
    return (o @ w_o).astype(x.dtype), k_new, v_new


def hier_moe(x, wr_g, br_g, wr_e, br_e, w_gate, w_up, w_down):
    B, T, D = x.shape
    f32 = jnp.float32
    xt = x.reshape(-1, D)
    gl = (xt @ wr_g + br_g).astype(f32)
    grp = jnp.argmax(gl, -1)
    p_grp = jnp.take_along_axis(jax.nn.softmax(gl, -1), grp[:, None], -1)
    el = (jnp.einsum('nd,gde->nge', xt, wr_e) + br_e).astype(f32)
    el = jnp.take_along_axis(el, grp[:, None, None], 1)[:, 0]
    top_v, top_i = lax.top_k(el, TOPK_IN_GROUP)
    top_w = jax.nn.softmax(top_v, -1)
    w_in = jnp.sum(jax.nn.one_hot(top_i, EXPERTS_PER_GROUP, dtype=f32) * top_w[..., None], axis=1)
    gate = (jax.nn.one_hot(grp, N_GROUPS, dtype=f32)[:, :, None] * (p_grp * w_in)[:, None, :]).reshape(-1, N_EXPERTS)
    h = jax.nn.silu(jnp.einsum('nd,edf->nef', xt, w_gate)) * jnp.einsum('nd,edf->nef', xt, w_up)
    y = (h * gate[:, :, None].astype(h.dtype)).reshape(xt.shape[0], -1) @ w_down.reshape(-1, D)
    return y.reshape(B, T, D).astype(x.dtype)


def setup_inputs(seed: int = 0) -> dict:
    key = jax.random.key(seed)
    kit = iter(jax.random.split(key, 64))

    def nrm(shape, scale):
        return scale * jax.random.normal(next(kit), shape, jnp.float32)

    def unif(shape, lo, hi):
        return jax.random.uniform(next(kit), shape, jnp.float32, lo, hi)

    D = D_MODEL
    s = D ** -0.5
    beta = DEEPNORM_BETA
    n_pages = PAST_LEN // PAGE_SIZE
    n_pool = (DEC_BATCH * n_pages * 5) // 4
    NA, NBL = N_A_LAYERS, N_B_LAYERS
    H, N = RWKV_HEADS, RWKV_HEAD
    rkv_scale = jnp.array([1.0, 1.0, beta], jnp.float32)[None, :, None, None]
    qkv_scale = jnp.concatenate([jnp.ones((2 * D,), jnp.float32), jnp.full((D,), beta, jnp.float32)])
    return {
        'x_prompt': nrm((BATCH, SEQ, D), 1.0),
        'x_sample': nrm((DEC_BATCH, DEC_SEQ, D), 1.0),
        'cache_k': nrm((NBL, n_pool, PAGE_SIZE, MOBA_HEADS, MOBA_HEAD), 1.0),
        'cache_v': nrm((NBL, n_pool, PAGE_SIZE, MOBA_HEADS, MOBA_HEAD), beta),
        'state_wkv': nrm((NA, DEC_BATCH, H, N, N), 0.3),
        'state_shift': nrm((NA, DEC_BATCH, D), 1.0),
        'page_table': jax.random.permutation(next(kit), n_pool)[:DEC_BATCH * n_pages].reshape(DEC_BATCH, n_pages).astype(jnp.int32),
        'ln1_g': 1.0 + nrm((DEPTH, D), 0.02),
        'ln1_b': nrm((DEPTH, D), 0.02),
        'ln2_g': 1.0 + nrm((DEPTH, D), 0.02),
        'ln2_b': nrm((DEPTH, D), 0.02),
        'rw_mu': unif((NA, 6, D), 0.0, 1.0),
        'rw_w_rkv': nrm((NA, 3, D, D), s) * rkv_scale,
        'rw_w0': unif((NA, D), -6.0, 0.0),
        'rw_w1': nrm((NA, D, DECAY_LORA), s),
        'rw_w2': nrm((NA, DECAY_LORA, D), 0.1 * DECAY_LORA ** -0.5),
        'rw_a0': nrm((NA, D), 0.1),
        'rw_a1': nrm((NA, D, AAA_LORA), s),
        'rw_a2': nrm((NA, AAA_LORA, D), 0.1 * AAA_LORA ** -0.5),
        'rw_v0': 1.0 + nrm((NA - 1, D), 0.1),
        'rw_v1': nrm((NA - 1, D, MV_LORA), s),
        'rw_v2': nrm((NA - 1, MV_LORA, D), 0.1 * MV_LORA ** -0.5),
        'rw_g1': nrm((NA, D, GATE_LORA), s),
        'rw_g2': nrm((NA, GATE_LORA, D), GATE_LORA ** -0.5),
        'rw_k_k': 0.85 + nrm((NA, D), 0.02),
        'rw_k_a': 1.0 + nrm((NA, D), 0.02),
        'rw_r_k': nrm((NA, H, N), 0.1),
        'rw_gn_g': 1.0 + nrm((NA, D), 0.02),
        'rw_gn_b': nrm((NA, D), 0.02),
        'rw_w_o': nrm((NA, D, D), s * beta),
        'mb_w_qkv': nrm((NBL, D, 3 * D), s) * qkv_scale,
        'mb_w_o': nrm((NBL, D, D), s * beta),
        'rel_bias': nrm((REL_BUCKETS, MOBA_HEADS), 0.5),
        'moe_wr_g': nrm((DEPTH, D, N_GROUPS), s),
        'moe_br_g': nrm((DEPTH, N_GROUPS), 0.01),
        'moe_wr_e': nrm((DEPTH, N_GROUPS, D, EXPERTS_PER_GROUP), s),
        'moe_br_e': nrm((DEPTH, N_GROUPS, EXPERTS_PER_GROUP), 0.01),
        'moe_w_gate': nrm((DEPTH, N_EXPERTS, D, D_EXPERT), s),
        'moe_w_up': nrm((DEPTH, N_EXPERTS, D, D_EXPERT), s),
        'moe_w_down': nrm((DEPTH, N_EXPERTS, D_EXPERT, D), beta * D_EXPERT ** -0.5),
    }


def reference(x_prompt, x_sample, cache_k, cache_v, state_wkv, state_shift, page_table,
              ln1_g, ln1_b, ln2_g, ln2_b,
              rw_mu, rw_w_rkv, rw_w0, rw_w1, rw_w2, rw_a0, rw_a1, rw_a2, rw_v0, rw_v1, rw_v2,
              rw_g1, rw_g2, rw_k_k, rw_k_a, rw_r_k, rw_gn_g, rw_gn_b, rw_w_o,
              mb_w_qkv, mb_w_o, rel_bias,
              moe_wr_g, moe_br_g, moe_wr_e, moe_br_e, moe_w_gate, moe_w_up, moe_w_down):
    alpha = DEEPNORM_ALPHA
    xp, xs = x_prompt, x_sample
    Bp, Bs = xp.shape[0], xs.shape[0]
    n_pages = page_table.shape[1]
    vfirst_p = None
    vfirst_s = None
    kp_l, vp_l, ks_l, vs_l, Sp_l, Ss_l, shp_l, shs_l = [], [], [], [], [], [], [], []
    for i in range(DEPTH):
        if i % N_MIXERS == 0:
            a = i // N_MIXERS
            vres = None if a == 0 else (rw_v0[a - 1], rw_v1[a - 1], rw_v2[a - 1])
            prm = (rw_mu[a], rw_w_rkv[a], rw_w0[a], rw_w1[a], rw_w2[a], rw_a0[a], rw_a1[a], rw_a2[a],
                   rw_g1[a], rw_g2[a], rw_k_k[a], rw_k_a[a], rw_r_k[a], rw_gn_g[a], rw_gn_b[a], rw_w_o[a])
            s0_p = jnp.zeros((Bp, RWKV_HEADS, RWKV_HEAD, RWKV_HEAD), xp.dtype)
            mp, vfirst_p, Sp, shp = rwkv7_mixer(xp, jnp.zeros((Bp, D_MODEL), xp.dtype), s0_p, vfirst_p, vres, *prm)
            ms, vfirst_s, Ss, shs = rwkv7_mixer(xs, state_shift[a], state_wkv[a], vfirst_s, vres, *prm)
            Sp_l.append(Sp); Ss_l.append(Ss); shp_l.append(shp); shs_l.append(shs)
        else:
            b = i // N_MIXERS
            empty = jnp.zeros((Bp, 0, MOBA_HEADS, MOBA_HEAD), xp.dtype)
            mp, kp, vp = moba_mixer(xp, empty, empty, mb_w_qkv[b], mb_w_o[b], rel_bias)
            k_past = cache_k[b, page_table].reshape(Bs, n_pages * PAGE_SIZE, MOBA_HEADS, MOBA_HEAD)
            v_past = cache_v[b, page_table].reshape(Bs, n_pages * PAGE_SIZE, MOBA_HEADS, MOBA_HEAD)
            ms, ks, vs = moba_mixer(xs, k_past, v_past, mb_w_qkv[b], mb_w_o[b], rel_bias)
            kp_l.append(kp); vp_l.append(vp); ks_l.append(ks); vs_l.append(vs)
        xp = layer_norm(alpha * xp + mp, ln1_g[i], ln1_b[i])
        xs = layer_norm(alpha * xs + ms, ln1_g[i], ln1_b[i])
        moe = (moe_wr_g[i], moe_br_g[i], moe_wr_e[i], moe_br_e[i], moe_w_gate[i], moe_w_up[i], moe_w_down[i])
        xp = layer_norm(alpha * xp + hier_moe(xp, *moe), ln2_g[i], ln2_b[i])
        xs = layer_norm(alpha * xs + hier_moe(xs, *moe), ln2_g[i], ln2_b[i])
    return (xp, xs, jnp.stack(kp_l), jnp.stack(vp_l), jnp.stack(ks_l), jnp.stack(vs_l),
            jnp.stack(Sp_l), jnp.stack(Ss_l), jnp.stack(shp_l), jnp.stack(shs_l))
```

```python
import functools
import math

import jax
import jax.numpy as jnp
from jax import lax
from jax.experimental import pallas as pl
from jax.experimental.pallas import tpu as pltpu

F32 = jnp.float32
BF16 = jnp.bfloat16
HI = lax.Precision.HIGHEST

D_MODEL = 1024
DEPTH = 4
N_HEADS = 16
HEAD = 64
GN_EPS = 64e-5
LN_EPS = 1e-5
NEG_INF = -1e30
MOBA_BLOCK = 256
MOBA_TOPK = 3
REL_BUCKETS = 32
REL_MAX_DIST = 128
N_GROUPS = 4
EXPERTS_PER_GROUP = 8
N_EXPERTS = N_GROUPS * EXPERTS_PER_GROUP
D_EXPERT = 256
DEEPNORM_ALPHA = (2 * DEPTH) ** 0.25

SCAN_CHUNK = 64
VMEM_LIMIT = 56 * 1024 * 1024


def _params(sem):
    return pltpu.CompilerParams(dimension_semantics=sem, vmem_limit_bytes=VMEM_LIMIT)


def _bdot(a, b):
    return jnp.dot(a.astype(BF16), b.astype(BF16), preferred_element_type=F32)


def _bdot_nt(a, b):
    return lax.dot_general(a.astype(BF16), b.astype(BF16), (((1,), (1,)), ((), ())),
                           preferred_element_type=F32)


def _bdot_tn(a, b):
    return lax.dot_general(a.astype(BF16), b.astype(BF16), (((0,), (0,)), ((), ())),
                           preferred_element_type=F32)


def _hdot(a, b):
    return jnp.dot(a, b, preferred_element_type=F32, precision=HI)


def _round_bf16(x):
    return x.astype(BF16).astype(F32)


def _sigmoid(x):
    return 1.0 / (1.0 + jnp.exp(-x))


def _layer_norm(z, g, b):
    mu = jnp.mean(z, axis=-1, keepdims=True)
    c = z - mu
    var = jnp.mean(c * c, axis=-1, keepdims=True)
    return c * lax.rsqrt(var + LN_EPS) * g + b


def _head_sum(z, e, et):
    return _hdot(_hdot(z, e), et)


def _full(shape):
    nd = len(shape)
    return pl.BlockSpec(shape, lambda *_: (0,) * nd)


def _row_tile(n):
    return 256 if n % 256 == 0 else n


def _rwkv_proj_kernel(has_vres, *refs):
    if has_vres:
        (x_ref, xp_ref, vf_ref, mu_ref, wrkv_ref, w0_ref, w1_ref, w2_ref, a0_ref, a1_ref, a2_ref,
         v0_ref, v1_ref, v2_ref, g1_ref, g2_ref, kk_ref, ka_ref, rk_ref, e_ref, et_ref,
         r_out, ld_out, k_out, v_out, kk_out, kb_out, g_out, bonus_out) = refs
    else:
        (x_ref, xp_ref, mu_ref, wrkv_ref, w0_ref, w1_ref, w2_ref, a0_ref, a1_ref, a2_ref,
         g1_ref, g2_ref, kk_ref, ka_ref, rk_ref, e_ref, et_ref,
         r_out, ld_out, k_out, v_out, kk_out, kb_out, g_out, bonus_out) = refs
    x = x_ref[...]
    dx = xp_ref[...] - x
    mix = lambda p: x + dx * mu_ref[p:p + 1, :]
    r = _bdot(mix(0), wrkv_ref[0])
    k = _bdot(mix(1), wrkv_ref[1])
    xv = mix(2)
    v = _bdot(xv, wrkv_ref[2])
    zw = w0_ref[...] + _bdot(jnp.tanh(_bdot(mix(3), w1_ref[...])), w2_ref[...])
    w = -(jnp.maximum(-zw, 0.0) + jnp.log(1.0 + jnp.exp(-jnp.abs(zw)))) - 0.5
    ld_out[...] = -jnp.exp(w)
    a = _sigmoid(a0_ref[...] + _bdot(_bdot(mix(4), a1_ref[...]), a2_ref[...]))
    g_out[...] = _bdot(_sigmoid(_bdot(mix(5), g1_ref[...])), g2_ref[...])
    if has_vres:
        gate = _sigmoid(v0_ref[...] + _bdot(_bdot(xv, v1_ref[...]), v2_ref[...]))
        v = v + (vf_ref[...] - v) * gate
    e, et = e_ref[...], et_ref[...]
    kk = k * kk_ref[...]
    kk = kk * lax.rsqrt(jnp.maximum(_head_sum(kk * kk, e, et), 1e-24))
    k = k * (1.0 + (a - 1.0) * ka_ref[...])
    r_out[...] = r
    k_out[...] = k
    v_out[...] = v
    kk_out[...] = kk
    kb_out[...] = kk * a
    bonus_out[...] = _head_sum(r * k * rk_ref[...], e, et) * v


def _rwkv_proj(x, x_prev, v_first, p, e, et):
    n, d = x.shape
    tm = _row_tile(n)
    has_vres = v_first is not None
    row = pl.BlockSpec((tm, d), lambda i: (i, 0))
    acts = [x, x_prev] + ([v_first] if has_vres else [])
    weights = [p["mu"], p["w_rkv"], p["w0"], p["w1"], p["w2"], p["a0"], p["a1"], p["a2"]]
    if has_vres:
        weights += [p["v0"], p["v1"], p["v2"]]
    weights += [p["g1"], p["g2"], p["k_k"], p["k_a"], p["r_k"], e, et]
    out = jax.ShapeDtypeStruct((n, d), F32)
    return pl.pallas_call(
        functools.partial(_rwkv_proj_kernel, has_vres),
        grid=(n // tm,),
        in_specs=[row] * len(acts) + [_full(w.shape) for w in weights],
        out_specs=[row] * 8,
        out_shape=[out] * 8,
        compiler_params=_params(("parallel",)),
        name="rwkv_proj",
    )(*acts, *weights)


def _rwkv_scan_kernel(r_ref, ld_ref, k_ref, v_ref, kk_ref, kb_ref, s0_ref, tri_ref,
                      o_ref, st_ref, s_scr):
    tc = r_ref.shape[1]
    c = pl.program_id(1)

    @pl.when(c == 0)
    def _():
        s_scr[...] = s0_ref[0]

    ld = ld_ref[0]
    lc = _hdot(tri_ref[...], ld)
    last = lc[tc - 1:tc, :]
    inv = jnp.exp(-lc)
    hat = jnp.exp(last - lc)
    kkv, kbv, kv, v = kk_ref[0], kb_ref[0], k_ref[0], v_ref[0]
    rt = r_ref[0] * jnp.exp(lc)
    at = -kkv * jnp.exp(lc - ld)
    bt = kbv * inv
    kt = kv * inv
    bh = kbv * hat
    kh = kv * hat
    wl = jnp.exp(last)
    rows = lax.broadcasted_iota(jnp.int32, (tc, tc), 0)
    cols = lax.broadcasted_iota(jnp.int32, (tc, tc), 1)
    strict = rows > cols
    incl = rows >= cols
    n_lvl = int(math.log2(tc))
    for h in range(N_HEADS):
        sl = slice(h * HEAD, (h + 1) * HEAD)
        ar = jnp.concatenate([at[:, sl], rt[:, sl]], axis=0)
        bk = jnp.concatenate([bt[:, sl], kt[:, sl]], axis=0)
        pm = _bdot_nt(ar, bk)
        s0 = s_scr[h]
        q = _bdot_nt(ar, s0)
        a_ab = jnp.where(strict, pm[:tc, :tc], 0.0)
        a_ak = jnp.where(strict, pm[:tc, tc:], 0.0)
        a_rb = jnp.where(incl, pm[tc:, :tc], 0.0)
        a_rk = jnp.where(incl, pm[tc:, tc:], 0.0)
        vh = v[:, sl]
        u = q[:tc] + _bdot(a_ak, vh)
        apow = a_ab
        for lvl in range(n_lvl):
            u = u + _bdot(apow, u)
            if lvl + 1 < n_lvl:
                apow = _bdot(apow, apow)
        o_ref[0, :, sl] = q[tc:] + _bdot(a_rb, u) + _bdot(a_rk, vh)
        uv = jnp.concatenate([u, vh], axis=0)
        bkh = jnp.concatenate([bh[:, sl], kh[:, sl]], axis=0)
        s_scr[h] = s0 * wl[:, sl] + _bdot_tn(uv, bkh)

    @pl.when(c == pl.num_programs(1) - 1)
    def _():
        st_ref[0] = s_scr[...]


def _rwkv_scan(r, ld, k, v, kk, kb, s0):
    b, t, d = r.shape
    tc = SCAN_CHUNK
    tri = (jnp.arange(tc)[:, None] >= jnp.arange(tc)[None, :]).astype(F32)
    blk = pl.BlockSpec((1, tc, d), lambda i, c: (i, c, 0))
    st = pl.BlockSpec((1, N_HEADS, HEAD, HEAD), lambda i, c: (i, 0, 0, 0))
    return pl.pallas_call(
        _rwkv_scan_kernel,
        grid=(b, t // tc),
        in_specs=[blk] * 6 + [st, _full(tri.shape)],
        out_specs=[blk, st],
        out_shape=[jax.ShapeDtypeStruct((b, t, d), F32),
                   jax.ShapeDtypeStruct((b, N_HEADS, HEAD, HEAD), F32)],
        scratch_shapes=[pltpu.VMEM((N_HEADS, HEAD, HEAD), F32)],
        compiler_params=_params(("parallel", "arbitrary")),
        name="rwkv_scan",
    )(r, ld, k, v, kk, kb, s0, tri)


def _rwkv_step_kernel(r_ref, ld_ref, k_ref, v_ref, kk_ref, kb_ref, s0_ref, o_ref, st_ref):
    nb = r_ref.shape[0]
    eye = (lax.broadcasted_iota(jnp.int32, (HEAD, HEAD), 0)
           == lax.broadcasted_iota(jnp.int32, (HEAD, HEAD), 1)).astype(F32)

    def body(i, carry):
        for h in range(N_HEADS):
            hs = pl.ds(h, 1)
            s = s0_ref[i, h]
            kk_r, kb_r, k_r = kk_ref[i, hs, :], kb_ref[i, hs, :], k_ref[i, hs, :]
            w_r = jnp.exp(ld_ref[i, hs, :])
            sk = jnp.sum(s * kk_r, axis=1, keepdims=True)
            v_c = jnp.sum(eye * v_ref[i, hs, :], axis=1, keepdims=True)
            s_new = s * w_r - sk * kb_r + v_c * k_r
            o_c = jnp.sum(s_new * r_ref[i, hs, :], axis=1, keepdims=True)
            o_ref[i, hs, :] = jnp.sum(eye * o_c, axis=0, keepdims=True)
            st_ref[i, h] = s_new
        return carry

    lax.fori_loop(0, nb, body, 0)


def _rwkv_step(r, ld, k, v, kk, kb, s0):
    n = r.shape[0]
    nb = 8
    heads = lambda z: z.reshape(n, N_HEADS, HEAD)
    vec = pl.BlockSpec((nb, N_HEADS, HEAD), lambda i: (i, 0, 0))
    st = pl.BlockSpec((nb, N_HEADS, HEAD, HEAD), lambda i: (i, 0, 0, 0))
    o, s_t = pl.pallas_call(
        _rwkv_step_kernel,
        grid=(n // nb,),
        in_specs=[vec] * 6 + [st],
        out_specs=[vec, st],
        out_shape=[jax.ShapeDtypeStruct((n, N_HEADS, HEAD), F32),
                   jax.ShapeDtypeStruct((n, N_HEADS, HEAD, HEAD), F32)],
        compiler_params=_params(("parallel",)),
        name="rwkv_step",
    )(*(heads(z) for z in (r, ld, k, v, kk, kb)), s0)
    return o.reshape(n, N_HEADS * HEAD), s_t


def _rwkv_out_kernel(o_ref, bonus_ref, g_ref, x_ref, gng_ref, gnb_ref, wo_ref, lng_ref, lnb_ref,
                     e_ref, et_ref, out_ref):
    e, et = e_ref[...], et_ref[...]
    o = o_ref[...]
    c = o - _head_sum(o, e, et) * (1.0 / HEAD)
    var = _head_sum(c * c, e, et) * (1.0 / HEAD)
    on = c * lax.rsqrt(var + GN_EPS) * gng_ref[...] + gnb_ref[...]
    y = _bdot((on + bonus_ref[...]) * g_ref[...], wo_ref[...])
    out_ref[...] = _layer_norm(DEEPNORM_ALPHA * x_ref[...] + y, lng_ref[...], lnb_ref[...])


def _rwkv_out(o, bonus, g, x, p, ln_g, ln_b, e, et):
    n, d = x.shape
    tm = _row_tile(n)
    row = pl.BlockSpec((tm, d), lambda i: (i, 0))
    weights = [p["gn_g"], p["gn_b"], p["w_o"], ln_g, ln_b, e, et]
    return pl.pallas_call(
        _rwkv_out_kernel,
        grid=(n // tm,),
        in_specs=[row] * 4 + [_full(w.shape) for w in weights],
        out_specs=row,
        out_shape=jax.ShapeDtypeStruct((n, d), F32),
        compiler_params=_params(("parallel",)),
        name="rwkv_out",
    )(o, bonus, g, x, *weights)


def _attn_out_kernel(o_ref, x_ref, wo_ref, lng_ref, lnb_ref, out_ref):
    y = _bdot(o_ref[...], wo_ref[...])
    out_ref[...] = _layer_norm(DEEPNORM_ALPHA * x_ref[...] + y, lng_ref[...], lnb_ref[...])


def _attn_out(o, x, w_o, ln_g, ln_b):
    n, d = x.shape
    tm = _row_tile(n)
    row = pl.BlockSpec((tm, d), lambda i: (i, 0))
    weights = [w_o, ln_g, ln_b]
    return pl.pallas_call(
        _attn_out_kernel,
        grid=(n // tm,),
        in_specs=[row] * 2 + [_full(w.shape) for w in weights],
        out_specs=row,
        out_shape=jax.ShapeDtypeStruct((n, d), F32),
        compiler_params=_params(("parallel",)),
        name="attn_out",
    )(o, x, *weights)


def _qkv_kernel(x_ref, w_ref, q_ref, k_ref, v_ref, kb_ref, vb_ref, km_ref):
    d = x_ref.shape[1]
    qkv = _bdot(x_ref[...], w_ref[...])
    k = qkv[:, d:2 * d]
    v = qkv[:, 2 * d:]
    q_ref[...] = qkv[:, :d]
    k_ref[...] = k
    v_ref[...] = v
    kb_ref[...] = k.astype(BF16)
    vb_ref[...] = v.astype(BF16)
    km_ref[0] = jnp.mean(k, axis=0, keepdims=True)


def _qkv(x, w_qkv):
    n, d = x.shape
    tm = _row_tile(n)
    row = pl.BlockSpec((tm, d), lambda i: (i, 0))
    f = jax.ShapeDtypeStruct((n, d), F32)
    h = jax.ShapeDtypeStruct((n, d), BF16)
    return pl.pallas_call(
        _qkv_kernel,
        grid=(n // tm,),
        in_specs=[row, _full(w_qkv.shape)],
        out_specs=[row] * 5 + [pl.BlockSpec((1, 1, d), lambda i: (i, 0, 0))],
        out_shape=[f, f, f, h, h, jax.ShapeDtypeStruct((n // tm, 1, d), F32)],
        compiler_params=_params(("parallel",)),
        name="moba_qkv",
    )(x, w_qkv)


def _top_mask(gate, valid, ids, axis, topk):
    g = jnp.where(valid, gate, NEG_INF)
    sel = jnp.zeros(gate.shape, jnp.bool_)
    big = jnp.int32(1 << 20)
    for _ in range(topk):
        mx = jnp.max(g, axis=axis, keepdims=True)
        first = jnp.min(jnp.where(g == mx, ids, big), axis=axis, keepdims=True)
        hit = ids == first
        sel = jnp.logical_or(sel, hit)
        g = jnp.where(hit, -jnp.inf, g)
    return jnp.logical_and(sel, valid)


def _moba_prompt_kernel(q_ref, k_ref, v_ref, km_ref, bias_ref, cfar_ref, o_ref, pen_ref):
    blk = MOBA_BLOCK
    i = pl.program_id(2)
    nblk = km_ref.shape[1]
    q2 = q_ref[0]
    km = km_ref[0]
    lane = lax.broadcasted_iota(jnp.int32, (1, 2 * HEAD), 1)
    bid = lax.broadcasted_iota(jnp.int32, (blk, nblk), 1)
    scale = HEAD ** -0.5
    outs = []
    for hh in range(2):
        mine = (lane // HEAD) == hh
        qm = jnp.where(mine, q2, 0.0)
        gate = _bdot_nt(qm, km)
        sel = _top_mask(gate, bid < i, bid, 1, MOBA_TOPK)
        for j in range(nblk):
            pen_ref[j] = jnp.broadcast_to(jnp.where(sel[:, j:j + 1], 0.0, NEG_INF), (blk, 2 * HEAD))
        qb = (qm * scale).astype(BF16)
        c_far = cfar_ref[0, hh:hh + 1, :]

        def scores(j):
            kj = k_ref[0, pl.ds(pl.multiple_of(j * blk, blk), blk), :]
            return lax.dot_general(qb, kj, (((1,), (1,)), ((), ())), preferred_element_type=F32)

        def values(j):
            return v_ref[0, pl.ds(pl.multiple_of(j * blk, blk), blk), :]

        def online(carry, s, vj):
            m, l, acc = carry
            m_new = jnp.maximum(m, jnp.max(s, axis=1, keepdims=True))
            alpha = jnp.exp(m - m_new)
            p = jnp.exp(s - m_new)
            l = alpha * l + jnp.sum(p, axis=1, keepdims=True)
            acc = alpha * acc + jnp.dot(p.astype(BF16), vj, preferred_element_type=F32)
            return m_new, l, acc

        s = scores(i) + bias_ref[0, hh, 0]
        m = jnp.max(s, axis=1, keepdims=True)
        p = jnp.exp(s - m)
        carry = (m, jnp.sum(p, axis=1, keepdims=True),
                 jnp.dot(p.astype(BF16), values(i), preferred_element_type=F32))

        def prev_block(carry):
            j = i - 1
            pen = pen_ref[j]
            s = scores(j) + bias_ref[0, hh, 1] + jnp.concatenate([pen, pen], axis=1)
            return online(carry, s, values(j))

        carry = lax.cond(i >= 1, prev_block, lambda cr: cr, carry)

        def far_block(j, carry):
            pen = pen_ref[j] + c_far
            s = scores(j) + jnp.concatenate([pen, pen], axis=1)
            return online(carry, s, values(j))

        m, l, acc = lax.fori_loop(0, jnp.maximum(i - 1, 0), far_block, carry)
        outs.append(acc / l)
    o_ref[0] = jnp.where((lane // HEAD) == 0, outs[0], outs[1])


def _t5_bucket(rel):
    n = jnp.maximum(rel, 0)
    max_exact = REL_BUCKETS // 2
    large = max_exact + (jnp.log(jnp.maximum(n, max_exact).astype(F32) / max_exact)
                         / math.log(REL_MAX_DIST / max_exact) * (REL_BUCKETS - max_exact)).astype(jnp.int32)
    large = jnp.minimum(large, REL_BUCKETS - 1)
    return jnp.where(n < max_exact, n, large)


def _moba_prompt(q, kb, vb, kmean, rel_bias):
    b, t, d = q.shape
    blk = MOBA_BLOCK
    nblk = t // blk
    npair = N_HEADS // 2
    tab = rel_bias[_t5_bucket(jnp.arange(2 * blk, dtype=jnp.int32))].T.astype(F32)
    diff = jnp.arange(blk, dtype=jnp.int32)[:, None] - jnp.arange(blk, dtype=jnp.int32)[None, :]
    own = jnp.where(diff >= 0, tab[:, jnp.maximum(diff, 0)], NEG_INF)
    prev = tab[:, diff + blk]
    bias = jnp.stack([own, prev], axis=1).reshape(npair, 2, 2, blk, blk)
    far = rel_bias[_t5_bucket(jnp.int32(2 * blk))].astype(F32)
    cfar = jnp.broadcast_to(far.reshape(npair, 2, 1), (npair, 2, 2 * HEAD))
    return pl.pallas_call(
        _moba_prompt_kernel,
        grid=(b, npair, nblk),
        in_specs=[pl.BlockSpec((1, blk, 2 * HEAD), lambda bi, p, i: (bi, i, p)),
                  pl.BlockSpec((1, t, 2 * HEAD), lambda bi, p, i: (bi, 0, p)),
                  pl.BlockSpec((1, t, 2 * HEAD), lambda bi, p, i: (bi, 0, p)),
                  pl.BlockSpec((1, nblk, 2 * HEAD), lambda bi, p, i: (bi, 0, p)),
                  pl.BlockSpec((1, 2, 2, blk, blk), lambda bi, p, i: (p, 0, 0, 0, 0)),
                  pl.BlockSpec((1, 2, 2 * HEAD), lambda bi, p, i: (p, 0, 0))],
        out_specs=pl.BlockSpec((1, blk, 2 * HEAD), lambda bi, p, i: (bi, i, p)),
        out_shape=jax.ShapeDtypeStruct((b, t, d), F32),
        scratch_shapes=[pltpu.VMEM((nblk, blk, 2 * HEAD), F32)],
        compiler_params=_params(("parallel", "parallel", "arbitrary")),
        name="moba_prompt",
    )(q, kb, vb, kmean, bias, cfar)


def _moba_sample_kernel(pt_ref, q_ref, k0_ref, k1_ref, v0_ref, v1_ref, kn_ref, vn_ref, bias_ref, b0_ref,
                        e_ref, et_ref, o_ref, m_scr, l_scr, g_scr, acc_scr):
    del pt_ref
    j = pl.program_id(1)
    nblk = pl.num_programs(1)
    scale = HEAD ** -0.5
    q = q_ref[0]
    e, et = e_ref[...], et_ref[...]
    kblk = jnp.concatenate([k0_ref[0], k1_ref[0]], axis=0)
    vblk = jnp.concatenate([v0_ref[0], v1_ref[0]], axis=0)
    kmean = jnp.mean(kblk, axis=0, keepdims=True)
    g_scr[pl.ds(j, 1), :] = _hdot(_round_bf16(q) * _round_bf16(kmean), e)
    s = _bdot(kblk * q, e) * scale + bias_ref[...]
    m = jnp.max(s, axis=0, keepdims=True)
    p = jnp.exp(s - m)
    m_scr[pl.ds(j, 1), :] = m
    l_scr[pl.ds(j, 1), :] = jnp.sum(p, axis=0, keepdims=True)
    acc_scr[pl.ds(j, 1), :] = jnp.sum(_bdot(p, et) * vblk, axis=0, keepdims=True)

    @pl.when(j == nblk - 1)
    def _():
        nb = g_scr.shape[0]
        ids = lax.broadcasted_iota(jnp.int32, (nb, N_HEADS), 0)
        sel = _top_mask(g_scr[...], ids >= 0, ids, 0, MOBA_TOPK)
        s_self = _hdot(q * kn_ref[0], e) * scale + b0_ref[...]
        mb = m_scr[...]
        m_all = jnp.maximum(jnp.max(jnp.where(sel, mb, NEG_INF), axis=0, keepdims=True), s_self)
        wb = jnp.where(sel, jnp.exp(mb - m_all), 0.0)
        w_self = jnp.exp(s_self - m_all)
        denom = jnp.sum(wb * l_scr[...], axis=0, keepdims=True) + w_self
        num = jnp.sum(_hdot(wb, et) * acc_scr[...], axis=0, keepdims=True) + _hdot(w_self, et) * vn_ref[0]
        o_ref[0] = num / _hdot(denom, et)


def _moba_sample(q, k_new, v_new, cache_k, cache_v, layer, page_table, rel_bias, e, et):
    n, d = q.shape
    n_pool, page = cache_k.shape[1], cache_k.shape[2]
    n_pages = page_table.shape[1]
    past = n_pages * page
    ppb = MOBA_BLOCK // page
    nblk = past // MOBA_BLOCK
    ck = cache_k.reshape(cache_k.shape[0] * n_pool, page, d)
    cv = cache_v.reshape(cache_v.shape[0] * n_pool, page, d)
    base = layer * n_pool
    rel = past - jnp.arange(past, dtype=jnp.int32)
    bias = rel_bias[_t5_bucket(rel)].astype(F32)
    b0 = rel_bias[0:1].astype(F32)
    vec = pl.BlockSpec((1, 1, d), lambda i, j, pt: (i, 0, 0))
    pg = lambda o: pl.BlockSpec((1, page, d), lambda i, j, pt: (base + pt[i, ppb * j + o], 0, 0))
    as3 = lambda z: z.reshape(n, 1, d)
    out = pl.pallas_call(
        _moba_sample_kernel,
        grid_spec=pltpu.PrefetchScalarGridSpec(
            num_scalar_prefetch=1,
            grid=(n, nblk),
            in_specs=[vec, pg(0), pg(1), pg(0), pg(1), vec, vec,
                      pl.BlockSpec((MOBA_BLOCK, N_HEADS), lambda i, j, pt: (j, 0)),
                      pl.BlockSpec((1, N_HEADS), lambda i, j, pt: (0, 0)),
                      pl.BlockSpec(e.shape, lambda i, j, pt: (0, 0)),
                      pl.BlockSpec(et.shape, lambda i, j, pt: (0, 0))],
            out_specs=vec,
            scratch_shapes=[pltpu.VMEM((nblk, N_HEADS), F32), pltpu.VMEM((nblk, N_HEADS), F32),
                            pltpu.VMEM((nblk, N_HEADS), F32), pltpu.VMEM((nblk, d), F32)]),
        out_shape=jax.ShapeDtypeStruct((n, 1, d), F32),
        compiler_params=_params(("parallel", "arbitrary")),
        name="moba_sample",
    )(page_table, as3(q), ck, ck, cv, cv, as3(k_new), as3(v_new), bias, b0, e, et)
    return out.reshape(n, d)


def _moe_gate(x, wrg, brg, wre, bre):
    gl = _bdot(x, wrg) + brg
    gid = lax.broadcasted_iota(jnp.int32, gl.shape, 1)
    gmax = jnp.max(gl, axis=1, keepdims=True)
    grp = jnp.min(jnp.where(gl == gmax, gid, N_GROUPS), axis=1, keepdims=True)
    p_grp = 1.0 / jnp.sum(jnp.exp(gl - gmax), axis=1, keepdims=True)
    el = _bdot(x, wre) + bre
    eid = lax.broadcasted_iota(jnp.int32, el.shape, 1)
    in_grp = (eid // EXPERTS_PER_GROUP) == grp
    g = jnp.where(in_grp, el, -jnp.inf)
    v1 = jnp.max(g, axis=1, keepdims=True)
    i1 = jnp.min(jnp.where(g == v1, eid, N_EXPERTS), axis=1, keepdims=True)
    g = jnp.where(eid == i1, -jnp.inf, g)
    v2 = jnp.max(g, axis=1, keepdims=True)
    i2 = jnp.min(jnp.where(g == v2, eid, N_EXPERTS), axis=1, keepdims=True)
    t = jnp.exp(v2 - v1)
    w1 = 1.0 / (1.0 + t)
    w2 = t / (1.0 + t)
    return p_grp * (jnp.where(eid == i1, w1, 0.0) + jnp.where(eid == i2, w2, 0.0))


def _moe_kernel(eg, x_ref, wrg_ref, brg_ref, wre_ref, bre_ref, wg_ref, wu_ref, wd_ref, lng_ref, lnb_ref,
                out_ref, xb_scr, gate_scr, acc_scr):
    s = pl.program_id(1)

    @pl.when(s == 0)
    def _():
        x = x_ref[...]
        xb_scr[...] = x.astype(BF16)
        gate_scr[...] = _moe_gate(x, wrg_ref[...], brg_ref[...], wre_ref[...], bre_ref[...])
        acc_scr[...] = jnp.zeros_like(acc_scr)

    xb = xb_scr[...]
    gate = gate_scr[...]
    eid = lax.broadcasted_iota(jnp.int32, gate.shape, 1)
    y = None
    for j in range(eg):
        gcol = jnp.sum(jnp.where(eid == s * eg + j, gate, 0.0), axis=1, keepdims=True)
        hg = jnp.dot(xb, wg_ref[j].astype(BF16), preferred_element_type=F32)
        hu = jnp.dot(xb, wu_ref[j].astype(BF16), preferred_element_type=F32)
        h = hg * _sigmoid(hg) * hu * gcol
        yj = jnp.dot(h.astype(BF16), wd_ref[j].astype(BF16), preferred_element_type=F32)
        y = yj if y is None else y + yj
    acc_scr[...] += y

    @pl.when(s == pl.num_programs(1) - 1)
    def _():
        out_ref[...] = _layer_norm(DEEPNORM_ALPHA * x_ref[...] + acc_scr[...], lng_ref[...], lnb_ref[...])


def _moe(x, p, ln_g, ln_b):
    n, d = x.shape
    tm = 1024 if n % 1024 == 0 else n
    eg = 2
    row = pl.BlockSpec((tm, d), lambda i, s: (i, 0))
    small = [p["wr_g"], p["br_g"], p["wr_e"], p["br_e"]]
    wspec = lambda w: pl.BlockSpec((eg,) + w.shape[1:], lambda i, s: (s, 0, 0))
    return pl.pallas_call(
        functools.partial(_moe_kernel, eg),
        grid=(n // tm, N_EXPERTS // eg),
        in_specs=[row] + [pl.BlockSpec(w.shape, lambda i, s: (0, 0)) for w in small]
        + [wspec(p["w_gate"]), wspec(p["w_up"]), wspec(p["w_down"])]
        + [pl.BlockSpec(ln_g.shape, lambda i, s: (0, 0))] * 2,
        out_specs=row,
        out_shape=jax.ShapeDtypeStruct((n, d), F32),
        scratch_shapes=[pltpu.VMEM((tm, d), BF16), pltpu.VMEM((tm, N_EXPERTS), F32), pltpu.VMEM((tm, d), F32)],
        compiler_params=_params(("parallel", "arbitrary")),
        name="hier_moe",
    )(x, *small, p["w_gate"], p["w_up"], p["w_down"], ln_g, ln_b)


def kernel(x_prompt, x_sample, cache_k, cache_v, state_wkv, state_shift, page_table, ln1_g, ln1_b, ln2_g, ln2_b, rw_mu, rw_w_rkv, rw_w0, rw_w1, rw_w2, rw_a0, rw_a1, rw_a2, rw_v0, rw_v1, rw_v2, rw_g1, rw_g2, rw_k_k, rw_k_a, rw_r_k, rw_gn_g, rw_gn_b, rw_w_o, mb_w_qkv, mb_w_o, rel_bias, moe_wr_g, moe_br_g, moe_wr_e, moe_br_e, moe_w_gate, moe_w_up, moe_w_down):
    bp, t, d = x_prompt.shape
    bs = x_sample.shape[0]
    row = lambda z: z.reshape(1, -1)
    hid = jnp.arange(d, dtype=jnp.int32) // HEAD
    e = (hid[:, None] == jnp.arange(N_HEADS, dtype=jnp.int32)[None, :]).astype(F32)
    et = e.T

    xp = x_prompt.reshape(bp * t, d)
    xs = x_sample.reshape(bs, d)
    vfirst_p = vfirst_s = None
    kp_l, vp_l, ks_l, vs_l, sp_l, ss_l, shp_l, shs_l = [], [], [], [], [], [], [], []
    for i in range(DEPTH):
        ln1 = (row(ln1_g[i]), row(ln1_b[i]))
        if i % 2 == 0:
            a = i // 2
            p = dict(mu=rw_mu[a], w_rkv=rw_w_rkv[a].astype(BF16), w0=row(rw_w0[a]), w1=rw_w1[a].astype(BF16),
                     w2=rw_w2[a].astype(BF16), a0=row(rw_a0[a]), a1=rw_a1[a].astype(BF16), a2=rw_a2[a].astype(BF16),
                     g1=rw_g1[a].astype(BF16), g2=rw_g2[a].astype(BF16), k_k=row(rw_k_k[a]), k_a=row(rw_k_a[a]),
                     r_k=row(rw_r_k[a]), gn_g=row(rw_gn_g[a]), gn_b=row(rw_gn_b[a]), w_o=rw_w_o[a].astype(BF16))
            if a > 0:
                p.update(v0=row(rw_v0[a - 1]), v1=rw_v1[a - 1].astype(BF16), v2=rw_v2[a - 1].astype(BF16))
            x3 = xp.reshape(bp, t, d)
            x_prev = jnp.concatenate([jnp.zeros((bp, 1, d), F32), x3[:, :-1]], axis=1).reshape(bp * t, d)
            r, ld, k, v, kk, kb, g, bonus = _rwkv_proj(xp, x_prev, vfirst_p, p, e, et)
            if a == 0:
                vfirst_p = v
            as3 = lambda z: z.reshape(bp, t, d)
            o, s_t = _rwkv_scan(as3(r), as3(ld), as3(k), as3(v), as3(kk), as3(kb),
                                jnp.zeros((bp, N_HEADS, HEAD, HEAD), F32))
            sp_l.append(s_t)
            shp_l.append(x3[:, -1])
            xp = _rwkv_out(o.reshape(bp * t, d), bonus, g, xp, p, *ln1, e, et)
            r, ld, k, v, kk, kb, g, bonus = _rwkv_proj(xs, state_shift[a], vfirst_s, p, e, et)
            if a == 0:
                vfirst_s = v
            o, s_t = _rwkv_step(r, ld, k, v, kk, kb, state_wkv[a])
            ss_l.append(s_t)
            shs_l.append(xs)
            xs = _rwkv_out(o, bonus, g, xs, p, *ln1, e, et)
        else:
            b = i // 2
            w_qkv = mb_w_qkv[b].astype(BF16)
            w_o = mb_w_o[b].astype(BF16)
            q, k, v, kb16, vb16, kmean = _qkv(xp, w_qkv)
            kp_l.append(k.reshape(bp, t, N_HEADS, HEAD))
            vp_l.append(v.reshape(bp, t, N_HEADS, HEAD))
            as3 = lambda z: z.reshape(bp, t, d)
            o = _moba_prompt(as3(q), as3(kb16), as3(vb16), kmean.reshape(bp, t // MOBA_BLOCK, d), rel_bias)
            xp = _attn_out(o.reshape(bp * t, d), xp, w_o, *ln1)
            q, k, v, _, _, _ = _qkv(xs, w_qkv)
            ks_l.append(k.reshape(bs, 1, N_HEADS, HEAD))
            vs_l.append(v.reshape(bs, 1, N_HEADS, HEAD))
            o = _moba_sample(q, k, v, cache_k, cache_v, b, page_table, rel_bias, e, et)
            xs = _attn_out(o, xs, w_o, *ln1)
        moe = dict(wr_g=moe_wr_g[i], br_g=row(moe_br_g[i]),
                   wr_e=moe_wr_e[i].transpose(1, 0, 2).reshape(d, N_EXPERTS), br_e=row(moe_br_e[i]),
                   w_gate=moe_w_gate[i], w_up=moe_w_up[i], w_down=moe_w_down[i])
        ln2 = (row(ln2_g[i]), row(ln2_b[i]))
        xp = _moe(xp, moe, *ln2)
        xs = _moe(xs, moe, *ln2)
    return (xp.reshape(bp, t, d), xs.reshape(bs, 1, d),
            jnp.stack(kp_l), jnp.stack(vp_l), jnp.stack(ks_l), jnp.stack(vs_l),
            jnp.stack(sp_l), jnp.stack(ss_l), jnp.stack(shp_l), jnp.stack(shs_l))
```
